```python
import jax, jax.numpy as jnp
from jax import lax
import numpy as np

D_MODEL = 2048
BATCH = 2
SEQ = 4096
DEPTH = 4
DEC_BATCH = 8
DEC_SEQ = 4
PAST_LEN = 16384
PAGE_SIZE = 128

N_A_LAYERS = DEPTH // 2
N_B_LAYERS = DEPTH - N_A_LAYERS
POOL_WINDOWS = (2, 4, 8, 16)
N_POOL_GROUPS = len(POOL_WINDOWS)
POOL_GROUP = D_MODEL // N_POOL_GROUPS
POOL_STATE = max(POOL_WINDOWS) - 1
N_HEADS = 16
HEAD_DIM = D_MODEL // N_HEADS
D_FF = ((8 * D_MODEL // 3 + 127) // 128) * 128
Q_BLOCK = 128
RMS_EPS = 1e-6

kernel_name = "yoco_pool_fox_macaron_step"


def _rmsnorm(x, g):
    xf = x.astype(jnp.float32)
    y = xf * lax.rsqrt(jnp.mean(xf * xf, axis=-1, keepdims=True) + RMS_EPS)
    return (y * g.astype(jnp.float32)).astype(x.dtype)


def _half_ffn(x, g, w_gate, w_up, w_down):
    h = _rmsnorm(x, g)
    return x + 0.5 * ((jax.nn.silu(h @ w_gate) * (h @ w_up)) @ w_down)


def _pool_mix(u, prev, w, scale):
    n_prev = prev.shape[1]
    full = jnp.concatenate([prev, u], axis=1).astype(jnp.float32)
    length = full.shape[1]
    idx = jnp.arange(length, dtype=jnp.int32)
    cs = jnp.cumsum(full, axis=1)
    outs = []
    for g, win in enumerate(POOL_WINDOWS):
        sl = slice(g * POOL_GROUP, (g + 1) * POOL_GROUP)
        c = cs[..., sl]
        c_shift = jnp.concatenate([jnp.zeros_like(c[:, :win]), c[:, :-win]], axis=1)
        cnt = jnp.minimum(idx + 1, win).astype(jnp.float32)[None, :, None]
        d = ((c - c_shift) / cnt - full[..., sl])[:, n_prev:]
        outs.append(jnp.einsum('bli,io->blo', d, w[g].astype(jnp.float32)))
    return (jnp.concatenate(outs, axis=-1) * scale.astype(jnp.float32)).astype(u.dtype)


def _shared_kv(h, norm_kv, w_kv, w_f, b_f):
    bsz, length, _ = h.shape
    hn = _rmsnorm(h, norm_kv)
    kv = hn @ w_kv
    k = kv[..., :D_MODEL].reshape(bsz, length, N_HEADS, HEAD_DIM)
    v = kv[..., D_MODEL:].reshape(bsz, length, N_HEADS, HEAD_DIM)
    logf = jax.nn.log_sigmoid((hn @ w_f).astype(jnp.float32) + b_f.astype(jnp.float32))
    return k, v, logf


def _fox_attend(q, q_f, q_pos, segments):
    qs = q.astype(jnp.float32) * (HEAD_DIM ** -0.5)
    qf = jnp.swapaxes(q_f.astype(jnp.float32), 1, 2)[..., :, None]
    logits = []
    for k, _, k_f, k_pos in segments:
        s = jnp.einsum('bqhd,bkhd->bhqk', qs, k.astype(jnp.float32))
        s = s + qf - jnp.swapaxes(k_f.astype(jnp.float32), 1, 2)[..., None, :]
        logits.append(jnp.where(k_pos[None, :] <= q_pos[:, None], s, -jnp.inf))
    probs = jax.nn.softmax(jnp.concatenate(logits, axis=-1), axis=-1)
    outs = []
    start = 0
    for k, v, _, _ in segments:
        n = k.shape[1]
        outs.append(jnp.einsum('bhqk,bkhd->bqhd', probs[..., start:start + n],
                               v.astype(jnp.float32)))
        start += n
    return sum(outs[1:], outs[0]).astype(q.dtype)


def _prompt_attend_builder(k_new, v_new, logf_new):
    f_cum = jnp.cumsum(logf_new, axis=1)
    length = k_new.shape[1]
    pos = jnp.arange(length, dtype=jnp.int32)
    n_blocks = length // Q_BLOCK

    def attend(q):
        bsz = q.shape[0]
        qb = jnp.swapaxes(q.reshape(bsz, n_blocks, Q_BLOCK, N_HEADS, HEAD_DIM), 0, 1)
        fb = jnp.swapaxes(f_cum.reshape(bsz, n_blocks, Q_BLOCK, N_HEADS), 0, 1)
        pb = pos.reshape(n_blocks, Q_BLOCK)
        ob = lax.map(lambda a: _fox_attend(a[0], a[1], a[2],
                                           ((k_new, v_new, f_cum, pos),)), (qb, fb, pb))
        return jnp.swapaxes(ob, 0, 1).reshape(bsz, length, N_HEADS, HEAD_DIM)
    return attend


def _sample_attend_builder(k_past, v_past, logf_past):
    past_len = k_past.shape[1]
    logf_past = logf_past.astype(jnp.float32)
    kf_past = logf_past - lax.cumsum(logf_past, axis=1, reverse=True)
    pos_past = jnp.arange(past_len, dtype=jnp.int32)

    def build(k_new, v_new, logf_new):
        g_cum = jnp.cumsum(logf_new, axis=1)
        pos_new = past_len + jnp.arange(k_new.shape[1], dtype=jnp.int32)

        def attend(q):
            return _fox_attend(q, g_cum, pos_new,
                               ((k_past, v_past, kf_past, pos_past),
                                (k_new, v_new, g_cum, pos_new)))
        return attend
    return build


def _trunk(x, pool_prev, build_attend, norm_ffn1, norm_mix, norm_ffn2, ffn_gate, ffn_up,
           ffn_down, pool_w, pool_scale, norm_kv, w_kv, w_f, b_f, w_q, w_o, norm_final):
    bsz, length, _ = x.shape
    pool_new = []
    attend = None
    k_new = v_new = logf_new = None
    for layer in range(DEPTH):
        x = _half_ffn(x, norm_ffn1[layer], ffn_gate[layer, 0], ffn_up[layer, 0], ffn_down[layer, 0])
        u = _rmsnorm(x, norm_mix[layer])
        if layer < N_A_LAYERS:
            x = x + _pool_mix(u, pool_prev[layer], pool_w[layer], pool_scale[layer])
            pool_new.append(jnp.concatenate([pool_prev[layer], u], axis=1)[:, -POOL_STATE:])
        else:
            j = layer - N_A_LAYERS
            q = (u @ w_q[j]).reshape(bsz, length, N_HEADS, HEAD_DIM)
            x = x + attend(q).reshape(bsz, length, D_MODEL) @ w_o[j]
        x = _half_ffn(x, norm_ffn2[layer], ffn_gate[layer, 1], ffn_up[layer, 1], ffn_down[layer, 1])
        if layer == N_A_LAYERS - 1:
            k_new, v_new, logf_new = _shared_kv(x, norm_kv, w_kv, w_f, b_f)
            attend = build_attend(k_new, v_new, logf_new)
    y = _rmsnorm(x, norm_final)
    return y, k_new, v_new, logf_new.astype(x.dtype), jnp.stack(pool_new)


def setup_inputs(seed: int = 0) -> dict:
    key = jax.random.key(seed)
    ks = jax.random.split(key, 24)
    f32 = jnp.float32
    n_pages = PAST_LEN // PAGE_SIZE
    n_used = DEC_BATCH * n_pages
    n_pool = n_used + max(1, n_used // 4)

    def nrm(k, shape, scale):
        return jax.random.normal(k, shape, f32) * scale

    b_f = jax.random.uniform(ks[0], (N_HEADS,), f32, 1.0, 6.0)
    page_table = jax.random.permutation(ks[1], n_pool)[:n_used].reshape(
        DEC_BATCH, n_pages).astype(jnp.int32)
    return {
        "x_prompt": nrm(ks[2], (BATCH, SEQ, D_MODEL), 1.0),
        "x_sample": nrm(ks[3], (DEC_BATCH, DEC_SEQ, D_MODEL), 1.0),
        "cache_k": nrm(ks[4], (n_pool, PAGE_SIZE, N_HEADS, HEAD_DIM), 1.0),
        "cache_v": nrm(ks[5], (n_pool, PAGE_SIZE, N_HEADS, HEAD_DIM), 1.0),
        "cache_logf": jax.nn.log_sigmoid(b_f + nrm(ks[6], (n_pool, PAGE_SIZE, N_HEADS), 1.0)),
        "state_pool": nrm(ks[7], (N_A_LAYERS, DEC_BATCH, POOL_STATE, D_MODEL), 1.0),
        "page_table": page_table,
        "norm_ffn1": 1.0 + nrm(ks[8], (DEPTH, D_MODEL), 0.05),
        "norm_mix": 1.0 + nrm(ks[9], (DEPTH, D_MODEL), 0.05),
        "norm_ffn2": 1.0 + nrm(ks[10], (DEPTH, D_MODEL), 0.05),
        "ffn_gate": nrm(ks[11], (DEPTH, 2, D_MODEL, D_FF), D_MODEL ** -0.5),
        "ffn_up": nrm(ks[12], (DEPTH, 2, D_MODEL, D_FF), D_MODEL ** -0.5),
        "ffn_down": nrm(ks[13], (DEPTH, 2, D_FF, D_MODEL), D_FF ** -0.5),
        "pool_w": nrm(ks[14], (N_A_LAYERS, N_POOL_GROUPS, POOL_GROUP, POOL_GROUP), POOL_GROUP ** -0.5),
        "pool_scale": 1.0 + nrm(ks[15], (N_A_LAYERS, D_MODEL), 0.1),
        "norm_kv": 1.0 + nrm(ks[16], (D_MODEL,), 0.05),
        "w_kv": nrm(ks[17], (D_MODEL, 2 * D_MODEL), D_MODEL ** -0.5),
        "w_f": nrm(ks[18], (D_MODEL, N_HEADS), 0.5 * D_MODEL ** -0.5),
        "b_f": b_f,
        "w_q": nrm(ks[19], (N_B_LAYERS, D_MODEL, D_MODEL), D_MODEL ** -0.5),
        "w_o": nrm(ks[20], (N_B_LAYERS, D_MODEL, D_MODEL), D_MODEL ** -0.5),
        "norm_final": 1.0 + nrm(ks[21], (D_MODEL,), 0.05),
    }


def reference(x_prompt, x_sample, cache_k, cache_v, cache_logf, state_pool, page_table,
              norm_ffn1, norm_mix, norm_ffn2, ffn_gate, ffn_up, ffn_down, pool_w, pool_scale,
              norm_kv, w_kv, w_f, b_f, w_q, w_o, norm_final):
    pool_prev_p = jnp.zeros((N_A_LAYERS, x_prompt.shape[0], 0, D_MODEL), x_prompt.dtype)
    y_prompt, k_prompt, v_prompt, logf_prompt, pool_prompt = _trunk(
        x_prompt, pool_prev_p, _prompt_attend_builder, norm_ffn1, norm_mix, norm_ffn2,
        ffn_gate, ffn_up, ffn_down, pool_w, pool_scale, norm_kv, w_kv, w_f, b_f, w_q, w_o,
        norm_final)

    dec_b = x_sample.shape[0]
    k_past = cache_k[page_table].reshape(dec_b, -1, N_HEADS, HEAD_DIM)
    v_past = cache_v[page_table].reshape(dec_b, -1, N_HEADS, HEAD_DIM)
    logf_past = cache_logf[page_table].reshape(dec_b, -1, N_HEADS)
    y_sample, k_sample, v_sample, logf_sample, pool_sample = _trunk(
        x_sample, state_pool, _sample_attend_builder(k_past, v_past, logf_past),
        norm_ffn1, norm_mix, norm_ffn2, ffn_gate, ffn_up, ffn_down, pool_w, pool_scale,
        norm_kv, w_kv, w_f, b_f, w_q, w_o, norm_final)

    return (y_prompt, y_sample, k_prompt, v_prompt, logf_prompt, pool_prompt,
            k_sample, v_sample, logf_sample, pool_sample)
```

```python
import functools

import jax
import jax.numpy as jnp
from jax import lax
from jax.experimental import pallas as pl
from jax.experimental.pallas import tpu as pltpu

F32 = jnp.float32
BF16 = jnp.bfloat16

RMS_EPS = 1e-6
POOL_WINDOWS = (2, 4, 8, 16)
POOL_HALO = 16
LANES = 128
VMEM_LIMIT = 56 * 1024 * 1024


def _cparams(*sem):
    return pltpu.CompilerParams(dimension_semantics=sem, vmem_limit_bytes=VMEM_LIMIT)


def _rms(x, g):
    return x * lax.rsqrt(jnp.mean(x * x, axis=-1, keepdims=True) + RMS_EPS) * g


def _mm(a, b):
    return jnp.dot(a, b, preferred_element_type=F32)


def _mm_nt(a, b):
    return lax.dot_general(a, b, (((1,), (1,)), ((), ())), preferred_element_type=F32)


def _ffn_kernel(x_ref, g_ref, wg_ref, wu_ref, wd_ref, o_ref, h_ref, *, tf, tail):
    f = pl.program_id(1)
    nf = pl.num_programs(1)

    @pl.when(f == 0)
    def _():
        h_ref[...] = _rms(x_ref[...], g_ref[...]).astype(BF16)
        o_ref[...] = jnp.zeros_like(o_ref)

    def accumulate(width):
        h = h_ref[...]
        gate = _mm(h, wg_ref[:, :width].astype(BF16))
        up = _mm(h, wu_ref[:, :width].astype(BF16))
        act = (gate * jax.nn.sigmoid(gate) * up).astype(BF16)
        o_ref[...] += _mm(act, wd_ref[:width, :].astype(BF16))

    if tail == tf:
        accumulate(tf)
    else:
        pl.when(f < nf - 1)(lambda: accumulate(tf))
        pl.when(f == nf - 1)(lambda: accumulate(tail))

    @pl.when(f == nf - 1)
    def _():
        o_ref[...] = x_ref[...] + 0.5 * o_ref[...]


def _half_ffn(x, g, w_gate, w_up, w_down, *, tm, tf):
    m, d = x.shape
    d_ff = w_gate.shape[1]
    nf = pl.cdiv(d_ff, tf)
    tail = d_ff - (nf - 1) * tf
    return pl.pallas_call(
        functools.partial(_ffn_kernel, tf=tf, tail=tail),
        grid=(m // tm, nf),
        in_specs=[
            pl.BlockSpec((tm, d), lambda i, f: (i, 0), pipeline_mode=pl.Buffered(1)),
            pl.BlockSpec((1, d), lambda i, f: (0, 0)),
            pl.BlockSpec((d, tf), lambda i, f: (0, f)),
            pl.BlockSpec((d, tf), lambda i, f: (0, f)),
            pl.BlockSpec((tf, d), lambda i, f: (f, 0)),
        ],
        out_specs=pl.BlockSpec((tm, d), lambda i, f: (i, 0)),
        out_shape=jax.ShapeDtypeStruct((m, d), F32),
        scratch_shapes=[pltpu.VMEM((tm, d), BF16)],
        compiler_params=_cparams("parallel", "arbitrary"),
        name="half_ffn",
    )(x, g.reshape(1, d), w_gate, w_up, w_down)


def _q_kernel(x_ref, g_ref, w_ref, o_ref, h_ref, *, scale):
    @pl.when(pl.program_id(1) == 0)
    def _():
        h_ref[...] = _rms(x_ref[...], g_ref[...]).astype(BF16)

    o_ref[...] = (_mm(h_ref[...], w_ref[...].astype(BF16)) * scale).astype(o_ref.dtype)


def _q_proj(x, g, w, *, scale, out_dtype, tm, tn):
    m, d = x.shape
    n = w.shape[1]
    return pl.pallas_call(
        functools.partial(_q_kernel, scale=scale),
        grid=(m // tm, n // tn),
        in_specs=[
            pl.BlockSpec((tm, d), lambda i, j: (i, 0)),
            pl.BlockSpec((1, d), lambda i, j: (0, 0)),
            pl.BlockSpec((d, tn), lambda i, j: (0, j)),
        ],
        out_specs=pl.BlockSpec((tm, tn), lambda i, j: (i, j)),
        out_shape=jax.ShapeDtypeStruct((m, n), out_dtype),
        scratch_shapes=[pltpu.VMEM((tm, d), BF16)],
        compiler_params=_cparams("parallel", "arbitrary"),
        name="q_proj",
    )(x, g.reshape(1, d), w)


def _log_sigmoid(z):
    return jnp.minimum(z, 0.0) - jnp.log1p(jnp.exp(-jnp.abs(z)))


def _kv_kernel(x_ref, g_ref, wk_ref, wv_ref, wf_ref, bf_ref,
               k32_ref, v32_ref, k16_ref, v16_ref, lf_ref, h_ref):
    @pl.when(pl.program_id(1) == 0)
    def _():
        h = _rms(x_ref[...], g_ref[...]).astype(BF16)
        h_ref[...] = h
        lf_ref[...] = _log_sigmoid(_mm(h, wf_ref[...].astype(BF16)) + bf_ref[...])

    h = h_ref[...]
    k = _mm(h, wk_ref[...].astype(BF16))
    v = _mm(h, wv_ref[...].astype(BF16))
    k32_ref[...] = k
    v32_ref[...] = v
    k16_ref[...] = k.astype(BF16)
    v16_ref[...] = v.astype(BF16)


def _kv_proj(x, g, w_kv, w_f, b_f, *, tm, tn):
    m, d = x.shape
    n_heads = w_f.shape[1]
    nk = d // tn
    wf_pad = jnp.pad(w_f, ((0, 0), (0, LANES - n_heads)))
    bf_pad = jnp.pad(b_f, (0, LANES - n_heads)).reshape(1, LANES)
    tile = pl.BlockSpec((tm, tn), lambda i, j: (i, j))
    return pl.pallas_call(
        _kv_kernel,
        grid=(m // tm, nk),
        in_specs=[
            pl.BlockSpec((tm, d), lambda i, j: (i, 0)),
            pl.BlockSpec((1, d), lambda i, j: (0, 0)),
            pl.BlockSpec((d, tn), lambda i, j: (0, j)),
            pl.BlockSpec((d, tn), lambda i, j: (0, j + nk)),
            pl.BlockSpec((d, LANES), lambda i, j: (0, 0)),
            pl.BlockSpec((1, LANES), lambda i, j: (0, 0)),
        ],
        out_specs=[tile, tile, tile, tile, pl.BlockSpec((tm, LANES), lambda i, j: (i, 0))],
        out_shape=[
            jax.ShapeDtypeStruct((m, d), F32),
            jax.ShapeDtypeStruct((m, d), F32),
            jax.ShapeDtypeStruct((m, d), BF16),
            jax.ShapeDtypeStruct((m, d), BF16),
            jax.ShapeDtypeStruct((m, LANES), F32),
        ],
        scratch_shapes=[pltpu.VMEM((tm, d), BF16)],
        compiler_params=_cparams("parallel", "arbitrary"),
        name="kv_proj",
    )(x, g.reshape(1, d), w_kv, w_kv, wf_pad, bf_pad)


def _o_kernel(a_ref, w_ref, x_ref, o_ref):
    o_ref[...] = x_ref[...] + _mm(a_ref[...].astype(BF16), w_ref[...].astype(BF16))


def _o_proj(a, w, x, *, tm, tn):
    m, d = x.shape
    return pl.pallas_call(
        _o_kernel,
        grid=(m // tm, d // tn),
        in_specs=[
            pl.BlockSpec((tm, d), lambda i, j: (i, 0)),
            pl.BlockSpec((d, tn), lambda i, j: (0, j)),
            pl.BlockSpec((tm, tn), lambda i, j: (i, j)),
        ],
        out_specs=pl.BlockSpec((tm, tn), lambda i, j: (i, j)),
        out_shape=jax.ShapeDtypeStruct((m, d), F32),
        compiler_params=_cparams("parallel", "arbitrary"),
        name="o_proj",
    )(a, w, x)


def _norm_kernel(x_ref, g_ref, o_ref):
    o_ref[...] = _rms(x_ref[...], g_ref[...])


def _final_norm(x, g, *, tm):
    m, d = x.shape
    return pl.pallas_call(
        _norm_kernel,
        grid=(m // tm,),
        in_specs=[pl.BlockSpec((tm, d), lambda i: (i, 0)), pl.BlockSpec((1, d), lambda i: (0, 0))],
        out_specs=pl.BlockSpec((tm, d), lambda i: (i, 0)),
        out_shape=jax.ShapeDtypeStruct((m, d), F32),
        compiler_params=_cparams("parallel"),
        name="final_norm",
    )(x, g.reshape(1, d))


def _pool_prompt_kernel(x_ref, halo_ref, g_ref, w_ref, sc_ref, o_ref, st_ref, u_ref, *, tm, tiles_per_seq):
    t = pl.program_id(0) % tiles_per_seq
    g = g_ref[...]
    x = x_ref[...]
    u = _rms(x, g)
    u_ref[pl.ds(POOL_HALO, tm), :] = u
    u_ref[pl.ds(0, POOL_HALO), :] = jnp.where(t == 0, 0.0, _rms(halo_ref[...], g))
    st_ref[0] = u[tm - POOL_HALO:, :]

    pos = t * tm + lax.broadcasted_iota(jnp.int32, (tm, 1), 0)
    n_grp = len(POOL_WINDOWS)
    gw = x.shape[1] // n_grp
    for gi, win in enumerate(POOL_WINDOWS):
        cols = slice(gi * gw, (gi + 1) * gw)
        s = u_ref[pl.ds(POOL_HALO, tm), cols]
        for k in range(1, win):
            s = s + u_ref[pl.ds(POOL_HALO - k, tm), cols]
        inv_cnt = 1.0 / jnp.minimum(pos + 1, win).astype(F32)
        dlt = s * inv_cnt - u[:, cols]
        y = _mm(dlt.astype(BF16), w_ref[gi].astype(BF16))
        o_ref[:, cols] = x[:, cols] + y * sc_ref[:, cols]


def _pool_prompt(x, g, w, scale, *, seq, tm):
    m, d = x.shape
    n_seq = m // seq
    tiles_per_seq = seq // tm
    halo_blocks = tm // POOL_HALO
    return pl.pallas_call(
        functools.partial(_pool_prompt_kernel, tm=tm, tiles_per_seq=tiles_per_seq),
        grid=(m // tm,),
        in_specs=[
            pl.BlockSpec((tm, d), lambda i: (i, 0)),
            pl.BlockSpec((POOL_HALO, d), lambda i: (jnp.maximum(i * halo_blocks - 1, 0), 0)),
            pl.BlockSpec((1, d), lambda i: (0, 0)),
            pl.BlockSpec(w.shape, lambda i: (0, 0, 0)),
            pl.BlockSpec((1, d), lambda i: (0, 0)),
        ],
        out_specs=[
            pl.BlockSpec((tm, d), lambda i: (i, 0)),
            pl.BlockSpec((1, POOL_HALO, d), lambda i: (i // tiles_per_seq, 0, 0)),
        ],
        out_shape=[
            jax.ShapeDtypeStruct((m, d), F32),
            jax.ShapeDtypeStruct((n_seq, POOL_HALO, d), F32),
        ],
        scratch_shapes=[pltpu.VMEM((POOL_HALO + tm, d), F32)],
        compiler_params=_cparams("arbitrary"),
        name="pool_prompt",
    )(x, x, g.reshape(1, d), w, scale.reshape(1, d))


def _pool_sample_kernel(x_ref, prev_ref, g_ref, w_ref, sc_ref, o_ref, st_ref, u_ref, d_ref, *, n_new):
    n_seq, n_prev, _ = prev_ref.shape
    x = x_ref[...]
    u_ref[...] = _rms(x, g_ref[...])
    n_grp = len(POOL_WINDOWS)
    gw = x.shape[1] // n_grp

    def full_row(b, idx):
        if idx >= n_prev:
            return u_ref[pl.ds(b * n_new + idx - n_prev, 1), :]
        return prev_ref[b, pl.ds(idx, 1), :]

    for b in range(n_seq):
        for t in range(n_new):
            r = b * n_new + t
            cur = full_row(b, n_prev + t)
            acc = cur
            for k in range(1, max(POOL_WINDOWS)):
                acc = acc + full_row(b, n_prev + t - k)
                if (k + 1) in POOL_WINDOWS:
                    gi = POOL_WINDOWS.index(k + 1)
                    cols = slice(gi * gw, (gi + 1) * gw)
                    d_ref[pl.ds(r, 1), cols] = acc[:, cols] / float(k + 1) - cur[:, cols]
        for j in range(n_prev):
            st_ref[b, pl.ds(j, 1), :] = full_row(b, j + n_new)

    dlt = d_ref[...]
    for gi in range(n_grp):
        cols = slice(gi * gw, (gi + 1) * gw)
        y = _mm(dlt[:, cols].astype(BF16), w_ref[gi].astype(BF16))
        o_ref[:, cols] = x[:, cols] + y * sc_ref[:, cols]


def _pool_sample(x, prev, g, w, scale, *, n_new):
    m, d = x.shape
    assert prev.shape[1] == max(POOL_WINDOWS) - 1
    return pl.pallas_call(
        functools.partial(_pool_sample_kernel, n_new=n_new),
        out_shape=[jax.ShapeDtypeStruct((m, d), F32), jax.ShapeDtypeStruct(prev.shape, F32)],
        scratch_shapes=[pltpu.VMEM((m, d), F32), pltpu.VMEM((m, d), F32)],
        compiler_params=pltpu.CompilerParams(vmem_limit_bytes=VMEM_LIMIT),
        name="pool_sample",
    )(x, prev, g.reshape(1, d), w, scale.reshape(1, d))


def _lane_cumsum(x, *, reverse=False):
    n = x.shape[-1]
    lane = lax.broadcasted_iota(jnp.int32, x.shape, x.ndim - 1)
    k = 1
    while k < n:
        if reverse:
            x = x + jnp.where(lane < n - k, pltpu.roll(x, n - k, axis=x.ndim - 1), 0.0)
        else:
            x = x + jnp.where(lane >= k, pltpu.roll(x, k, axis=x.ndim - 1), 0.0)
        k *= 2
    return x


def _neg_cumsum_kernel(lf_ref, o_ref, *, n_heads):
    o_ref[0] = -_lane_cumsum(lf_ref[...].T[:n_heads, :])


def _neg_cum_logf(lf_pad, *, n_seq, seq, n_heads):
    return pl.pallas_call(
        functools.partial(_neg_cumsum_kernel, n_heads=n_heads),
        grid=(n_seq,),
        in_specs=[pl.BlockSpec((seq, LANES), lambda b: (b, 0))],
        out_specs=pl.BlockSpec((1, n_heads, seq), lambda b: (b, 0, 0)),
        out_shape=jax.ShapeDtypeStruct((n_seq, n_heads, seq), F32),
        compiler_params=_cparams("parallel"),
        name="neg_cum_logf",
    )(lf_pad)


def _flash_kernel(q_ref, k_ref, v_ref, b_ref, o_ref, *, tq):
    qi = pl.program_id(1)
    q = q_ref[...]
    dh = q.shape[1]

    def visit(j, carry, diagonal):
        m, l, acc = carry
        start = pl.multiple_of(j * tq, tq)
        s = _mm_nt(q, k_ref[pl.ds(start, tq), :]) + b_ref[:, pl.ds(start, tq)]
        if diagonal:
            row = lax.broadcasted_iota(jnp.int32, (tq, tq), 0)
            col = lax.broadcasted_iota(jnp.int32, (tq, tq), 1)
            s = jnp.where(col <= row, s, -jnp.inf)
        m_new = jnp.maximum(m, jnp.max(s, axis=-1, keepdims=True))
        alpha = jnp.exp(m - m_new)
        p = jnp.exp(s - m_new)
        l = alpha * l + jnp.sum(p, axis=-1, keepdims=True)
        acc = alpha * acc + _mm(p.astype(BF16), v_ref[pl.ds(start, tq), :])
        return m_new, l, acc

    init = (jnp.full((tq, 1), -jnp.inf, F32), jnp.zeros((tq, 1), F32), jnp.zeros((tq, dh), F32))
    carry = lax.fori_loop(0, qi, lambda j, c: visit(j, c, False), init)
    _, l, acc = visit(qi, carry, True)
    o_ref[...] = (acc / l).astype(o_ref.dtype)


def _flash_attend(q, k, v, neg_f, *, n_seq, seq, n_heads, tq):
    m, d = q.shape
    dh = d // n_heads
    nq = seq // tq
    bias = neg_f.reshape(n_seq * n_heads, 1, seq)
    kv_spec = pl.BlockSpec((seq, dh), lambda bh, i: (bh // n_heads, bh % n_heads))
    q_spec = pl.BlockSpec((tq, dh), lambda bh, i: ((bh // n_heads) * nq + i, bh % n_heads))
    return pl.pallas_call(
        functools.partial(_flash_kernel, tq=tq),
        grid=(n_seq * n_heads, nq),
        in_specs=[q_spec, kv_spec, kv_spec, pl.BlockSpec((None, 1, seq), lambda bh, i: (bh, 0, 0))],
        out_specs=q_spec,
        out_shape=jax.ShapeDtypeStruct((m, d), BF16),
        compiler_params=_cparams("parallel", "arbitrary"),
        name="flash_attend",
    )(q, k, v, bias)


def _decode_kernel(pt_ref, q_ref, kp_ref, vp_ref, lfp_ref, kn_ref, vn_ref, lfn_ref, o_ref,
                   qbd_ref, m_ref, l_ref, acc_ref, suf_ref, *, n_heads):
    del pt_ref
    p = pl.program_id(1)
    n_new = q_ref.shape[1]
    rows, d = qbd_ref.shape
    dh = d // n_heads
    page = kp_ref.shape[1]

    head_of_row = lax.broadcasted_iota(jnp.int32, (n_heads, d), 0)
    head_of_col = lax.broadcasted_iota(jnp.int32, (n_heads, d), 1) // dh

    @pl.when(p == 0)
    def _():
        for t in range(n_new):
            qrow = jnp.broadcast_to(q_ref[0, pl.ds(t, 1), :], (n_heads, d))
            qbd_ref[pl.ds(t * n_heads, n_heads), :] = jnp.where(head_of_row == head_of_col, qrow, 0.0).astype(BF16)
        m_ref[...] = jnp.full_like(m_ref, -jnp.inf)
        l_ref[...] = jnp.zeros_like(l_ref)
        acc_ref[...] = jnp.zeros_like(acc_ref)
        suf_ref[...] = jnp.zeros_like(suf_ref)

    def update(k, v, bias, mask):
        s = _mm_nt(qbd_ref[...], k.astype(BF16)) + jnp.concatenate([bias] * n_new, axis=0)
        if mask is not None:
            s = jnp.where(mask, s, -jnp.inf)
        m_prev = m_ref[...]
        m_new = jnp.maximum(m_prev, jnp.max(s, axis=-1, keepdims=True))
        alpha = jnp.exp(m_prev - m_new)
        pr = jnp.exp(s - m_new)
        l_ref[...] = alpha * l_ref[...] + jnp.sum(pr, axis=-1, keepdims=True)
        acc_ref[...] = alpha * acc_ref[...] + _mm(pr.astype(BF16), v.astype(BF16))
        m_ref[...] = m_new

    lf = lfp_ref[0]
    incl = _lane_cumsum(lf, reverse=True)
    suf = suf_ref[...]
    update(kp_ref[0], vp_ref[0], suf + (incl - lf), None)
    suf_ref[...] = suf + incl[:, 0:1]

    @pl.when(p == pl.num_programs(1) - 1)
    def _():
        g_cum = _lane_cumsum(lfn_ref[0].T[:n_heads, :])
        q_of_row = lax.broadcasted_iota(jnp.int32, (rows, page), 0) // n_heads
        key = lax.broadcasted_iota(jnp.int32, (rows, page), 1)
        update(kn_ref[0], vn_ref[0], -g_cum, key <= q_of_row)
        out = acc_ref[...] / l_ref[...]
        for t in range(n_new):
            blk = jnp.where(head_of_row == head_of_col, out[t * n_heads:(t + 1) * n_heads, :], 0.0)
            o_ref[0, pl.ds(t, 1), :] = jnp.sum(blk, axis=0, keepdims=True)


def _decode_attend(q, cache_k, cache_v, cache_lft, page_table, k_new, v_new, lf_new, *, n_heads):
    n_seq, n_new, d = q.shape
    page = cache_k.shape[1]
    n_pages = page_table.shape[1]
    rows = n_new * n_heads

    def past(b, p, pt):
        return (pt[b, n_pages - 1 - p], 0, 0)

    def own(b, p, pt):
        return (b, 0, 0)

    grid_spec = pltpu.PrefetchScalarGridSpec(
        num_scalar_prefetch=1,
        grid=(n_seq, n_pages),
        in_specs=[
            pl.BlockSpec((1, n_new, d), own),
            pl.BlockSpec((1, page, d), past),
            pl.BlockSpec((1, page, d), past),
            pl.BlockSpec((1, n_heads, page), past),
            pl.BlockSpec((1, page, d), own),
            pl.BlockSpec((1, page, d), own),
            pl.BlockSpec((1, page, LANES), own),
        ],
        out_specs=pl.BlockSpec((1, n_new, d), own),
        scratch_shapes=[
            pltpu.VMEM((rows, d), BF16),
            pltpu.VMEM((rows, 1), F32),
            pltpu.VMEM((rows, 1), F32),
            pltpu.VMEM((rows, d), F32),
            pltpu.VMEM((n_heads, 1), F32),
        ],
    )
    return pl.pallas_call(
        functools.partial(_decode_kernel, n_heads=n_heads),
        grid_spec=grid_spec,
        out_shape=jax.ShapeDtypeStruct((n_seq, n_new, d), F32),
        compiler_params=_cparams("parallel", "arbitrary"),
        name="decode_attend",
    )(page_table, q, cache_k, cache_v, cache_lft, k_new, v_new, lf_new)


def _trunk(x, seq, pool_step, build_attend, q_dtype, tiles, norm_ffn1, norm_mix, norm_ffn2, ffn_gate,
           ffn_up, ffn_down, norm_kv, w_kv, w_f, b_f, w_q, w_o, norm_final):
    m, d = x.shape
    depth = ffn_gate.shape[0]
    n_a = depth - w_q.shape[0]
    n_heads = w_f.shape[1]
    dh = d // n_heads
    tm, tf, tn = tiles
    states = []
    attend = None
    for layer in range(depth):
        x = _half_ffn(x, norm_ffn1[layer], ffn_gate[layer, 0], ffn_up[layer, 0], ffn_down[layer, 0], tm=tm, tf=tf)
        if layer < n_a:
            x, st = pool_step(layer, x)
            states.append(st)
        else:
            j = layer - n_a
            q = _q_proj(x, norm_mix[layer], w_q[j], scale=dh ** -0.5, out_dtype=q_dtype, tm=tm, tn=tn)
            x = _o_proj(attend(q), w_o[j], x, tm=tm, tn=tn)
        x = _half_ffn(x, norm_ffn2[layer], ffn_gate[layer, 1], ffn_up[layer, 1], ffn_down[layer, 1], tm=tm, tf=tf)
        if layer == n_a - 1:
            k32, v32, k16, v16, lf_pad = _kv_proj(x, norm_kv, w_kv, w_f, b_f, tm=tm, tn=tn // 2)
            attend = build_attend(k16, v16, k32, v32, lf_pad)
    y = _final_norm(x, norm_final, tm=tm)
    return y, k32, v32, lf_pad[:, :n_heads], jnp.stack(states)


def kernel(x_prompt, x_sample, cache_k, cache_v, cache_logf, state_pool, page_table, norm_ffn1, norm_mix, norm_ffn2, ffn_gate, ffn_up, ffn_down, pool_w, pool_scale, norm_kv, w_kv, w_f, b_f, w_q, w_o, norm_final):
    n_seq, seq, d = x_prompt.shape
    dec_b, dec_s, _ = x_sample.shape
    n_pool, page, n_heads, dh = cache_k.shape
    pool_state = state_pool.shape[2]
    weights = (norm_ffn1, norm_mix, norm_ffn2, ffn_gate, ffn_up, ffn_down, norm_kv, w_kv, w_f, b_f, w_q, w_o,
               norm_final)

    def pool_prompt(layer, x):
        x, st = _pool_prompt(x, norm_mix[layer], pool_w[layer], pool_scale[layer], seq=seq, tm=512)
        return x, st[:, POOL_HALO - pool_state:, :]

    def build_prompt_attend(k16, v16, k32, v32, lf_pad):
        neg_f = _neg_cum_logf(lf_pad, n_seq=n_seq, seq=seq, n_heads=n_heads)
        return lambda q: _flash_attend(q, k16, v16, neg_f, n_seq=n_seq, seq=seq, n_heads=n_heads, tq=512)

    y_p, k_p, v_p, lf_p, pool_p = _trunk(x_prompt.reshape(n_seq * seq, d), seq, pool_prompt, build_prompt_attend,
                                         BF16, (1024, 256, 512), *weights)

    def pool_sample(layer, x):
        return _pool_sample(x, state_pool[layer], norm_mix[layer], pool_w[layer], pool_scale[layer], n_new=dec_s)

    ck = cache_k.reshape(n_pool, page, d)
    cv = cache_v.reshape(n_pool, page, d)
    clft = jnp.swapaxes(cache_logf, 1, 2)

    def build_sample_attend(k16, v16, k32, v32, lf_pad):
        def pad_rows(a):
            a = a.reshape(dec_b, dec_s, a.shape[-1])
            return jnp.pad(a, ((0, 0), (0, page - dec_s), (0, 0)))
        k_new, v_new, lf_new = pad_rows(k32), pad_rows(v32), pad_rows(lf_pad)

        def attend(q):
            o = _decode_attend(q.reshape(dec_b, dec_s, d), ck, cv, clft, page_table, k_new, v_new, lf_new,
                               n_heads=n_heads)
            return o.reshape(dec_b * dec_s, d)
        return attend

    m_s = dec_b * dec_s
    y_s, k_s, v_s, lf_s, pool_s = _trunk(x_sample.reshape(m_s, d), dec_s, pool_sample, build_sample_attend,
                                         F32, (m_s, 512, 512), *weights)

    return (y_p.reshape(n_seq, seq, d), y_s.reshape(dec_b, dec_s, d),
            k_p.reshape(n_seq, seq, n_heads, dh), v_p.reshape(n_seq, seq, n_heads, dh),
            lf_p.reshape(n_seq, seq, n_heads), pool_p,
            k_s.reshape(dec_b, dec_s, n_heads, dh), v_s.reshape(dec_b, dec_s, n_heads, dh),
            lf_s.reshape(dec_b, dec_s, n_heads), pool_s)
```

```python
import functools

import jax
import jax.numpy as jnp
from jax import lax
from jax.experimental import pallas as pl
from jax.experimental.pallas import tpu as pltpu

F32 = jnp.float32
BF16 = jnp.bfloat16

RMS_EPS = 1e-6
POOL_WINDOWS = (2, 4, 8, 16)
POOL_HALO = 16
LANES = 128
SUBLANES = 8
VMEM_LIMIT = 60 * 1024 * 1024

PROMPT_TM, PROMPT_TF, PROMPT_TN = 1024, 512, 512
POOL_TM = 512
FLASH_TQ = 512
FLASH_HEADS = 4
SAMPLE_TF, SAMPLE_TN = 512, 512
DECODE_PAGES_PER_STEP = 4


def _cparams(*sem):
    return pltpu.CompilerParams(dimension_semantics=sem, vmem_limit_bytes=VMEM_LIMIT)


def _rms(x, g):
    return x * lax.rsqrt(jnp.mean(x * x, axis=-1, keepdims=True) + RMS_EPS) * g


def _mm(a, b):
    return jnp.dot(a, b, preferred_element_type=F32)


def _mm_nt(a, b):
    return lax.dot_general(a, b, (((1,), (1,)), ((), ())), preferred_element_type=F32)


def _vec_spec(layer, d, n_grid):
    if n_grid == 1:
        return pl.BlockSpec((None, 1, d), lambda i: (layer, 0, 0))
    return pl.BlockSpec((None, 1, d), lambda i, j: (layer, 0, 0))


def _ffn_kernel(x_ref, g_ref, wg_ref, wu_ref, wd_ref, *rest, tf, tail):
    *out_gain, o_ref, h_ref = rest
    f = pl.program_id(1)
    nf = pl.num_programs(1)

    @pl.when(f == 0)
    def _():
        x = x_ref[...]
        h_ref[...] = _rms(x, g_ref[...]).astype(BF16)
        o_ref[...] = x

    def accumulate(width):
        h = h_ref[...]
        gate = _mm(h, wg_ref[:, :width].astype(BF16))
        up = _mm(h, wu_ref[:, :width].astype(BF16))
        act = (gate * jax.nn.sigmoid(gate) * up * 0.5).astype(BF16)
        o_ref[...] += _mm(act, wd_ref[:width, :].astype(BF16))

    if tail == tf:
        accumulate(tf)
    else:
        pl.when(f < nf - 1)(lambda: accumulate(tf))
        pl.when(f == nf - 1)(lambda: accumulate(tail))

    if out_gain:
        @pl.when(f == nf - 1)
        def _():
            o_ref[...] = _rms(o_ref[...], out_gain[0][...])


def _half_ffn(x, g, layer, half, w_gate, w_up, w_down, *, tm, tf, out_gain=None):
    m, d = x.shape
    d_ff = w_gate.shape[-1]
    nf = pl.cdiv(d_ff, tf)
    tail = d_ff - (nf - 1) * tf
    extra_specs, extra_args = [], []
    if out_gain is not None:
        extra_specs = [pl.BlockSpec((1, d), lambda i, f: (0, 0))]
        extra_args = [out_gain.reshape(1, d)]
    return pl.pallas_call(
        functools.partial(_ffn_kernel, tf=tf, tail=tail),
        grid=(m // tm, nf),
        in_specs=[
            pl.BlockSpec((tm, d), lambda i, f: (i, 0), pipeline_mode=pl.Buffered(1)),
            _vec_spec(layer, d, 2),
            pl.BlockSpec((None, None, d, tf), lambda i, f: (layer, half, 0, f)),
            pl.BlockSpec((None, None, d, tf), lambda i, f: (layer, half, 0, f)),
            pl.BlockSpec((None, None, tf, d), lambda i, f: (layer, half, f, 0)),
        ] + extra_specs,
        out_specs=pl.BlockSpec((tm, d), lambda i, f: (i, 0)),
        out_shape=jax.ShapeDtypeStruct((m, d), F32),
        scratch_shapes=[pltpu.VMEM((tm, d), BF16)],
        compiler_params=_cparams("parallel", "arbitrary"),
        name="half_ffn",
    )(x, g, w_gate, w_up, w_down, *extra_args)


def _q_kernel(x_ref, g_ref, w_ref, o_ref, h_ref, *, scale):
    @pl.when(pl.program_id(1) == 0)
    def _():
        h_ref[...] = _rms(x_ref[...], g_ref[...]).astype(BF16)

    o_ref[...] = (_mm(h_ref[...], w_ref[...].astype(BF16)) * scale).astype(o_ref.dtype)


def _q_proj(x, g, layer, w, j, *, scale, out_dtype, tm, tn):
    m, d = x.shape
    n = w.shape[-1]
    return pl.pallas_call(
        functools.partial(_q_kernel, scale=scale),
        grid=(m // tm, n // tn),
        in_specs=[
            pl.BlockSpec((tm, d), lambda i, c: (i, 0)),
            _vec_spec(layer, d, 2),
            pl.BlockSpec((None, d, tn), lambda i, c: (j, 0, c)),
        ],
        out_specs=pl.BlockSpec((tm, tn), lambda i, c: (i, c)),
        out_shape=jax.ShapeDtypeStruct((m, n), out_dtype),
        scratch_shapes=[pltpu.VMEM((tm, d), BF16)],
        compiler_params=_cparams("parallel", "arbitrary"),
        name="q_proj",
    )(x, g, w)


def _log_sigmoid(z):
    return jnp.minimum(z, 0.0) - jnp.log1p(jnp.exp(-jnp.abs(z)))


def _kv_kernel(x_ref, g_ref, wk_ref, wv_ref, wf_ref, bf_ref,
               k32_ref, v32_ref, k16_ref, v16_ref, lf_ref, h_ref):
    @pl.when(pl.program_id(1) == 0)
    def _():
        h = _rms(x_ref[...], g_ref[...]).astype(BF16)
        h_ref[...] = h
        lf_ref[...] = _log_sigmoid(_mm(h, wf_ref[...].astype(BF16)) + bf_ref[...])

    h = h_ref[...]
    k = _mm(h, wk_ref[...].astype(BF16))
    v = _mm(h, wv_ref[...].astype(BF16))
    k32_ref[...] = k
    v32_ref[...] = v
    k16_ref[...] = k.astype(BF16)
    v16_ref[...] = v.astype(BF16)


def _kv_proj(x, g, w_kv, w_f, b_f, *, tm, tn):
    m, d = x.shape
    n_heads = w_f.shape[1]
    nk = d // tn
    wf_pad = jnp.pad(w_f, ((0, 0), (0, LANES - n_heads)))
    bf_pad = jnp.pad(b_f, (0, LANES - n_heads)).reshape(1, LANES)
    tile = pl.BlockSpec((tm, tn), lambda i, j: (i, j))
    return pl.pallas_call(
        _kv_kernel,
        grid=(m // tm, nk),
        in_specs=[
            pl.BlockSpec((tm, d), lambda i, j: (i, 0)),
            pl.BlockSpec((1, d), lambda i, j: (0, 0)),
            pl.BlockSpec((d, tn), lambda i, j: (0, j)),
            pl.BlockSpec((d, tn), lambda i, j: (0, j + nk)),
            pl.BlockSpec((d, LANES), lambda i, j: (0, 0)),
            pl.BlockSpec((1, LANES), lambda i, j: (0, 0)),
        ],
        out_specs=[tile, tile, tile, tile, pl.BlockSpec((tm, LANES), lambda i, j: (i, 0))],
        out_shape=[
            jax.ShapeDtypeStruct((m, d), F32),
            jax.ShapeDtypeStruct((m, d), F32),
            jax.ShapeDtypeStruct((m, d), BF16),
            jax.ShapeDtypeStruct((m, d), BF16),
            jax.ShapeDtypeStruct((m, LANES), F32),
        ],
        scratch_shapes=[pltpu.VMEM((tm, d), BF16)],
        compiler_params=_cparams("parallel", "arbitrary"),
        name="kv_proj",
    )(x, g.reshape(1, d), w_kv, w_kv, wf_pad, bf_pad)


def _o_kernel(a_ref, w_ref, x_ref, o_ref):
    o_ref[...] = x_ref[...] + _mm(a_ref[...].astype(BF16), w_ref[...].astype(BF16))


def _o_proj(a, w, j, x, *, tm, tn):
    m, d = x.shape
    return pl.pallas_call(
        _o_kernel,
        grid=(m // tm, d // tn),
        in_specs=[
            pl.BlockSpec((tm, d), lambda i, c: (i, 0)),
            pl.BlockSpec((None, d, tn), lambda i, c: (j, 0, c)),
            pl.BlockSpec((tm, tn), lambda i, c: (i, c)),
        ],
        out_specs=pl.BlockSpec((tm, tn), lambda i, c: (i, c)),
        out_shape=jax.ShapeDtypeStruct((m, d), F32),
        compiler_params=_cparams("parallel", "arbitrary"),
        name="o_proj",
    )(a, w, x)


def _pool_prompt_kernel(x_ref, halo_ref, g_ref, w_ref, sc_ref, o_ref, st_ref, u_ref, *, tm, tiles_per_seq):
    t = pl.program_id(0) % tiles_per_seq
    g = g_ref[...]
    x = x_ref[...]
    u = _rms(x, g)
    u_ref[pl.ds(POOL_HALO, tm), :] = u
    u_ref[pl.ds(0, POOL_HALO), :] = jnp.where(t == 0, 0.0, _rms(halo_ref[...], g))
    st_ref[0] = u[tm - POOL_HALO:, :]

    pos = t * tm + lax.broadcasted_iota(jnp.int32, (tm, 1), 0)
    n_grp = len(POOL_WINDOWS)
    gw = x.shape[1] // n_grp
    for gi, win in enumerate(POOL_WINDOWS):
        cols = slice(gi * gw, (gi + 1) * gw)
        s = u_ref[pl.ds(POOL_HALO, tm), cols]
        for k in range(1, win):
            s = s + u_ref[pl.ds(POOL_HALO - k, tm), cols]
        inv_cnt = 1.0 / jnp.minimum(pos + 1, win).astype(F32)
        dlt = s * inv_cnt - u[:, cols]
        y = _mm(dlt.astype(BF16), w_ref[gi].astype(BF16))
        o_ref[:, cols] = x[:, cols] + y * sc_ref[:, cols]


def _pool_prompt(x, g, w, scale, layer, *, seq, tm):
    m, d = x.shape
    n_seq = m // seq
    tiles_per_seq = seq // tm
    halo_blocks = tm // POOL_HALO
    return pl.pallas_call(
        functools.partial(_pool_prompt_kernel, tm=tm, tiles_per_seq=tiles_per_seq),
        grid=(m // tm,),
        in_specs=[
            pl.BlockSpec((tm, d), lambda i: (i, 0)),
            pl.BlockSpec((POOL_HALO, d), lambda i: (jnp.maximum(i * halo_blocks - 1, 0), 0)),
            _vec_spec(layer, d, 1),
            pl.BlockSpec((None,) + w.shape[1:], lambda i: (layer, 0, 0, 0)),
            _vec_spec(layer, d, 1),
        ],
        out_specs=[
            pl.BlockSpec((tm, d), lambda i: (i, 0)),
            pl.BlockSpec((1, POOL_HALO, d), lambda i: (i // tiles_per_seq, 0, 0)),
        ],
        out_shape=[
            jax.ShapeDtypeStruct((m, d), F32),
            jax.ShapeDtypeStruct((n_seq, POOL_HALO, d), F32),
        ],
        scratch_shapes=[pltpu.VMEM((POOL_HALO + tm, d), F32)],
        compiler_params=_cparams("arbitrary"),
        name="pool_prompt",
    )(x, x, g, w, scale)


def _pool_sample_kernel(x_ref, prev_ref, g_ref, w_ref, sc_ref, o_ref, st_ref, u_ref, d_ref, *, n_new):
    n_seq, n_prev, _ = prev_ref.shape
    x = x_ref[...]
    u_ref[...] = _rms(x, g_ref[...])
    n_grp = len(POOL_WINDOWS)
    gw = x.shape[1] // n_grp

    def full_row(b, idx):
        if idx >= n_prev:
            return u_ref[pl.ds(b * n_new + idx - n_prev, 1), :]
        return prev_ref[b, pl.ds(idx, 1), :]

    for b in range(n_seq):
        for t in range(n_new):
            r = b * n_new + t
            cur = full_row(b, n_prev + t)
            acc = cur
            for k in range(1, max(POOL_WINDOWS)):
                acc = acc + full_row(b, n_prev + t - k)
                if (k + 1) in POOL_WINDOWS:
                    gi = POOL_WINDOWS.index(k + 1)
                    cols = slice(gi * gw, (gi + 1) * gw)
                    d_ref[pl.ds(r, 1), cols] = acc[:, cols] / float(k + 1) - cur[:, cols]
        for j in range(n_prev):
            st_ref[b, pl.ds(j, 1), :] = full_row(b, j + n_new)

    dlt = d_ref[...]
    for gi in range(n_grp):
        cols = slice(gi * gw, (gi + 1) * gw)
        y = _mm(dlt[:, cols].astype(BF16), w_ref[gi].astype(BF16))
        o_ref[:, cols] = x[:, cols] + y * sc_ref[:, cols]


def _pool_sample(x, prev, g, w, scale, layer, *, n_new):
    m, d = x.shape
    assert prev.shape[2] == max(POOL_WINDOWS) - 1
    return pl.pallas_call(
        functools.partial(_pool_sample_kernel, n_new=n_new),
        grid=(1,),
        in_specs=[
            pl.BlockSpec((m, d), lambda i: (0, 0)),
            pl.BlockSpec((None,) + prev.shape[1:], lambda i: (layer, 0, 0, 0)),
            _vec_spec(layer, d, 1),
            pl.BlockSpec((None,) + w.shape[1:], lambda i: (layer, 0, 0, 0)),
            _vec_spec(layer, d, 1),
        ],
        out_specs=[
            pl.BlockSpec((m, d), lambda i: (0, 0)),
            pl.BlockSpec(prev.shape[1:], lambda i: (0, 0, 0)),
        ],
        out_shape=[jax.ShapeDtypeStruct((m, d), F32), jax.ShapeDtypeStruct(prev.shape[1:], F32)],
        scratch_shapes=[pltpu.VMEM((m, d), F32), pltpu.VMEM((m, d), F32)],
        compiler_params=_cparams("arbitrary"),
        name="pool_sample",
    )(x, prev, g, w, scale)


def _lane_cumsum(x, *, stride=1, reverse=False):
    n = x.shape[-1]
    axis = x.ndim - 1
    lane = lax.broadcasted_iota(jnp.int32, x.shape, axis)
    k = stride
    while k < n:
        if reverse:
            x = x + jnp.where(lane < n - k, pltpu.roll(x, n - k, axis=axis), 0.0)
        else:
            x = x + jnp.where(lane >= k, pltpu.roll(x, k, axis=axis), 0.0)
        k *= 2
    return x


def _lane_class_total(x, *, stride):
    n = x.shape[-1]
    k = stride
    while k < n:
        x = x + pltpu.roll(x, k, axis=x.ndim - 1)
        k *= 2
    return x


def _neg_cumsum_kernel(lf_ref, o_ref, *, n_heads):
    o_ref[0] = -_lane_cumsum(lf_ref[...].T[:n_heads, :])


def _neg_cum_logf(lf_pad, *, n_seq, seq, n_heads):
    return pl.pallas_call(
        functools.partial(_neg_cumsum_kernel, n_heads=n_heads),
        grid=(n_seq,),
        in_specs=[pl.BlockSpec((seq, LANES), lambda b: (b, 0))],
        out_specs=pl.BlockSpec((1, n_heads, seq), lambda b: (b, 0, 0)),
        out_shape=jax.ShapeDtypeStruct((n_seq, n_heads, seq), F32),
        compiler_params=_cparams("parallel"),
        name="neg_cum_logf",
    )(lf_pad)


def _flash_kernel(q_ref, k_ref, v_ref, b_ref, o_ref, *, tq, dh):
    qi = pl.program_id(1)
    n_grp = q_ref.shape[1] // dh

    def visit(j, carry, diagonal):
        start = pl.multiple_of(j * tq, tq)
        out = []
        for g in range(n_grp):
            m, l, acc = carry[3 * g:3 * g + 3]
            cols = slice(g * dh, (g + 1) * dh)
            s = _mm_nt(q_ref[:, cols], k_ref[pl.ds(start, tq), cols]) + b_ref[pl.ds(g, 1), pl.ds(start, tq)]
            if diagonal:
                row = lax.broadcasted_iota(jnp.int32, (tq, tq), 0)
                col = lax.broadcasted_iota(jnp.int32, (tq, tq), 1)
                s = jnp.where(col <= row, s, -jnp.inf)
            m_new = jnp.maximum(m, jnp.max(s, axis=-1, keepdims=True))
            alpha = jnp.exp(m - m_new)
            p = jnp.exp(s - m_new)
            l = alpha * l + jnp.sum(p, axis=-1, keepdims=True)
            acc = alpha * acc + _mm(p.astype(BF16), v_ref[pl.ds(start, tq), cols])
            out += [m_new, l, acc]
        return tuple(out)

    init = (jnp.full((tq, 1), -jnp.inf, F32), jnp.zeros((tq, 1), F32), jnp.zeros((tq, dh), F32)) * n_grp
    carry = lax.fori_loop(0, qi, lambda j, c: visit(j, c, False), init)
    carry = visit(qi, carry, True)
    for g in range(n_grp):
        _, l, acc = carry[3 * g:3 * g + 3]
        o_ref[:, g * dh:(g + 1) * dh] = (acc / l).astype(o_ref.dtype)


def _flash_attend(q, k, v, neg_f, *, n_seq, seq, n_heads, tq, heads_per_step):
    m, d = q.shape
    dh = d // n_heads
    nq = seq // tq
    hg = heads_per_step
    n_hg = n_heads // hg
    bias = neg_f.reshape(n_seq * n_hg, hg, seq)
    kv_spec = pl.BlockSpec((seq, hg * dh), lambda bh, i: (bh // n_hg, bh % n_hg))
    q_spec = pl.BlockSpec((tq, hg * dh), lambda bh, i: ((bh // n_hg) * nq + i, bh % n_hg))
    return pl.pallas_call(
        functools.partial(_flash_kernel, tq=tq, dh=dh),
        grid=(n_seq * n_hg, nq),
        in_specs=[q_spec, kv_spec, kv_spec, pl.BlockSpec((None, hg, seq), lambda bh, i: (bh, 0, 0))],
        out_specs=q_spec,
        out_shape=jax.ShapeDtypeStruct((m, d), BF16),
        compiler_params=_cparams("parallel", "arbitrary"),
        name="flash_attend",
    )(q, k, v, bias)


def _decode_kernel(pt_ref, q_ref, *refs, n_heads, pages_per_step):
    del pt_ref
    pps = pages_per_step
    kp_refs, vp_refs, lfp_refs = refs[:pps], refs[pps:2 * pps], refs[2 * pps:3 * pps]
    kn_ref, vn_ref, lfn_ref, o_ref, m_ref, l_ref, acc_ref, suf_ref, mask_ref = refs[3 * pps:]
    p = pl.program_id(1)
    rows, dh = q_ref.shape[1:]
    width = mask_ref.shape[1]
    q = q_ref[0].astype(BF16)

    @pl.when(p == 0)
    def _():
        row = lax.broadcasted_iota(jnp.int32, (rows, width), 0)
        col = lax.broadcasted_iota(jnp.int32, (rows, width), 1)
        mask_ref[...] = jnp.where((row % n_heads) == (col % n_heads), 0.0, -jnp.inf)
        m_ref[...] = jnp.full_like(m_ref, -jnp.inf)
        l_ref[...] = jnp.zeros_like(l_ref)
        acc_ref[...] = jnp.zeros_like(acc_ref)
        suf_ref[...] = jnp.zeros_like(suf_ref)

    def flat(ref):
        return ref[0].reshape(ref.shape[1] * n_heads, dh).astype(BF16)

    def update(logits, values):
        m_prev = m_ref[...]
        m_new = m_prev
        for s in logits:
            m_new = jnp.maximum(m_new, jnp.max(s, axis=-1, keepdims=True))
        alpha = jnp.exp(m_prev - m_new)
        l = alpha * l_ref[...]
        acc = alpha * acc_ref[...]
        for s, v in zip(logits, values):
            pr = jnp.exp(s - m_new)
            l = l + jnp.sum(pr, axis=-1, keepdims=True)
            acc = acc + _mm(pr.astype(BF16), v)
        l_ref[...] = l
        acc_ref[...] = acc
        m_ref[...] = m_new

    suf = suf_ref[...]
    logits = []
    for k_ref, lf_ref in zip(kp_refs, lfp_refs):
        lf = lf_ref[0]
        incl = _lane_cumsum(lf, stride=n_heads, reverse=True)
        logits.append(_mm_nt(q, flat(k_ref)) + (mask_ref[...] + (suf + (incl - lf))))
        suf = suf + _lane_class_total(lf, stride=n_heads)
    suf_ref[...] = suf
    update(logits, [flat(v_ref) for v_ref in vp_refs])

    @pl.when(p == pl.num_programs(1) - 1)
    def _():
        wn = kn_ref.shape[1] * n_heads
        row = lax.broadcasted_iota(jnp.int32, (rows, wn), 0)
        col = lax.broadcasted_iota(jnp.int32, (rows, wn), 1)
        keep = ((row % n_heads) == (col % n_heads)) & ((col // n_heads) <= (row // n_heads))
        s = _mm_nt(q, flat(kn_ref)) - _lane_cumsum(lfn_ref[0], stride=n_heads)
        update([jnp.where(keep, s, -jnp.inf)], [flat(vn_ref)])
        o_ref[0] = acc_ref[...] / l_ref[...]


def _decode_attend(q, cache_k, cache_v, cache_lf, page_table, k_new, v_new, lf_new):
    n_seq, rows, dh = q.shape
    _, page, n_heads, _ = cache_k.shape
    n_pages = page_table.shape[1]
    new_pad = k_new.shape[1]
    pps = DECODE_PAGES_PER_STEP
    assert n_pages % pps == 0

    def past(slot, ndim):
        return lambda b, p, pt: (pt[b, n_pages - 1 - pps * p - slot],) + (0,) * (ndim - 1)

    kv_specs = [pl.BlockSpec((1, page, n_heads, dh), past(s, 4)) for s in range(pps)]
    lf_specs = [pl.BlockSpec((1, 1, page * n_heads), past(s, 3)) for s in range(pps)]
    grid_spec = pltpu.PrefetchScalarGridSpec(
        num_scalar_prefetch=1,
        grid=(n_seq, n_pages // pps),
        in_specs=[pl.BlockSpec((1, rows, dh), lambda b, p, pt: (b, 0, 0))] + kv_specs + kv_specs + lf_specs + [
            pl.BlockSpec((1, new_pad, n_heads, dh), lambda b, p, pt: (b, 0, 0, 0)),
            pl.BlockSpec((1, new_pad, n_heads, dh), lambda b, p, pt: (b, 0, 0, 0)),
            pl.BlockSpec((1, 1, new_pad * n_heads), lambda b, p, pt: (b, 0, 0)),
        ],
        out_specs=pl.BlockSpec((1, rows, dh), lambda b, p, pt: (b, 0, 0)),
        scratch_shapes=[
            pltpu.VMEM((rows, 1), F32),
            pltpu.VMEM((rows, 1), F32),
            pltpu.VMEM((rows, dh), F32),
            pltpu.VMEM((1, page * n_heads), F32),
            pltpu.VMEM((rows, page * n_heads), F32),
        ],
    )
    return pl.pallas_call(
        functools.partial(_decode_kernel, n_heads=n_heads, pages_per_step=pps),
        grid_spec=grid_spec,
        out_shape=jax.ShapeDtypeStruct((n_seq, rows, dh), F32),
        compiler_params=_cparams("parallel", "arbitrary"),
        name="decode_attend",
    )(page_table, q, *([cache_k] * pps), *([cache_v] * pps), *([cache_lf] * pps), k_new, v_new, lf_new)


def _trunk(x, pool_step, build_attend, q_dtype, tiles, norm_ffn1, norm_mix, norm_ffn2, ffn_gate,
           ffn_up, ffn_down, norm_kv, w_kv, w_f, b_f, w_q, w_o, norm_final):
    m, d = x.shape
    depth = ffn_gate.shape[0]
    n_a = depth - w_q.shape[0]
    n_heads = w_f.shape[1]
    dh = d // n_heads
    tm, tf, tn = tiles
    ffn = functools.partial(_half_ffn, w_gate=ffn_gate, w_up=ffn_up, w_down=ffn_down, tm=tm, tf=tf)
    states = []
    attend = None
    for layer in range(depth):
        x = ffn(x, norm_ffn1, layer, 0)
        if layer < n_a:
            x, st = pool_step(layer, x)
            states.append(st)
        else:
            j = layer - n_a
            q = _q_proj(x, norm_mix, layer, w_q, j, scale=dh ** -0.5, out_dtype=q_dtype, tm=tm, tn=tn)
            x = _o_proj(attend(q), w_o, j, x, tm=tm, tn=tn)
        x = ffn(x, norm_ffn2, layer, 1, out_gain=norm_final if layer == depth - 1 else None)
        if layer == n_a - 1:
            k32, v32, k16, v16, lf_pad = _kv_proj(x, norm_kv, w_kv, w_f, b_f, tm=tm, tn=tn // 2)
            attend = build_attend(k16, v16, k32, v32, lf_pad)
    return x, k32, v32, lf_pad[:, :n_heads], jnp.stack(states)


def kernel(x_prompt, x_sample, cache_k, cache_v, cache_logf, state_pool, page_table, norm_ffn1, norm_mix, norm_ffn2, ffn_gate, ffn_up, ffn_down, pool_w, pool_scale, norm_kv, w_kv, w_f, b_f, w_q, w_o, norm_final):
    n_seq, seq, d = x_prompt.shape
    dec_b, dec_s, _ = x_sample.shape
    n_pool, page, n_heads, dh = cache_k.shape
    pool_state = state_pool.shape[2]

    def rows3(a):
        return a.reshape(a.shape[0], 1, a.shape[1])

    norm_mix3 = rows3(norm_mix)
    pool_scale3 = rows3(pool_scale)
    weights = (rows3(norm_ffn1), norm_mix3, rows3(norm_ffn2), ffn_gate, ffn_up, ffn_down, norm_kv, w_kv, w_f, b_f,
               w_q, w_o, norm_final)

    def pool_prompt(layer, x):
        x, st = _pool_prompt(x, norm_mix3, pool_w, pool_scale3, layer, seq=seq, tm=POOL_TM)
        return x, st[:, POOL_HALO - pool_state:, :]

    def build_prompt_attend(k16, v16, k32, v32, lf_pad):
        neg_f = _neg_cum_logf(lf_pad, n_seq=n_seq, seq=seq, n_heads=n_heads)
        return lambda q: _flash_attend(q, k16, v16, neg_f, n_seq=n_seq, seq=seq, n_heads=n_heads, tq=FLASH_TQ,
                                       heads_per_step=FLASH_HEADS)

    y_p, k_p, v_p, lf_p, pool_p = _trunk(x_prompt.reshape(n_seq * seq, d), pool_prompt, build_prompt_attend,
                                         BF16, (PROMPT_TM, PROMPT_TF, PROMPT_TN), *weights)

    def pool_sample(layer, x):
        return _pool_sample(x, state_pool, norm_mix3, pool_w, pool_scale3, layer, n_new=dec_s)

    cache_lf = cache_logf.reshape(n_pool, 1, page * n_heads)
    new_pad = -(-dec_s // SUBLANES) * SUBLANES

    def build_sample_attend(k16, v16, k32, v32, lf_pad):
        def pad_new(a, tail):
            a = a.reshape((dec_b, dec_s) + tail)
            return jnp.pad(a, ((0, 0), (0, new_pad - dec_s)) + ((0, 0),) * len(tail))
        k_new = pad_new(k32, (n_heads, dh))
        v_new = pad_new(v32, (n_heads, dh))
        lf_new = pad_new(lf_pad[:, :n_heads], (n_heads,)).reshape(dec_b, 1, new_pad * n_heads)

        def attend(q):
            o = _decode_attend(q.reshape(dec_b, dec_s * n_heads, dh), cache_k, cache_v, cache_lf, page_table,
                               k_new, v_new, lf_new)
            return o.reshape(dec_b * dec_s, d)
        return attend

    m_s = dec_b * dec_s
    y_s, k_s, v_s, lf_s, pool_s = _trunk(x_sample.reshape(m_s, d), pool_sample, build_sample_attend,
                                         F32, (m_s, SAMPLE_TF, SAMPLE_TN), *weights)

    return (y_p.reshape(n_seq, seq, d), y_s.reshape(dec_b, dec_s, d),
            k_p.reshape(n_seq, seq, n_heads, dh), v_p.reshape(n_seq, seq, n_heads, dh),
            lf_p.reshape(n_seq, seq, n_heads), pool_p,
            k_s.reshape(dec_b, dec_s, n_heads, dh), v_s.reshape(dec_b, dec_s, n_heads, dh),
            lf_s.reshape(dec_b, dec_s, n_heads), pool_s)
```
